```python
import numpy as np
import jax
import jax.numpy as jnp
from jax import lax

D_MODEL = 2048
BATCH = 8
SEQ = 2048
DEPTH = 1

GRID_W = 64
CTX_LEN = 256
EPS = 1e-6
NEG_INF = -1e30

NA_HEADS = 16
NA_HEAD_DIM = 64
NA_WIDTH = NA_HEADS * NA_HEAD_DIM
NA_WIN_H = 8
NA_WIN_W = 16
NA_QBLOCK_W = 16
NA_BAND_W = NA_QBLOCK_W + NA_WIN_W
NA_SCALE = NA_HEAD_DIM ** -0.5

MLA_HEADS = 8
MLA_NOPE_DIM = 128
MLA_ROPE_DIM = 64
MLA_V_DIM = 128
MLA_WIDTH = MLA_HEADS * MLA_V_DIM
MLA_Q_RANK = 768
MLA_KV_RANK = 512
MLA_Q_BLOCK = 128
MLA_SCALE = (MLA_NOPE_DIM + MLA_ROPE_DIM) ** -0.5
ROPE_THETA = 10000.0

PEER_HEADS = 8
PEER_N_KEYS = 128
PEER_N_EXPERTS = PEER_N_KEYS * PEER_N_KEYS
PEER_KEY_DIM = 256
PEER_TOPK = 16
PEER_TOKEN_BLOCK = 128

IN_SPLITS = (NA_WIDTH, NA_WIDTH, NA_WIDTH, MLA_Q_RANK, MLA_KV_RANK, MLA_ROPE_DIM, D_MODEL, D_MODEL)
IN_COLS = sum(IN_SPLITS)

kernel_name = 'hybrid_natten_mla_peer_dit_block'


def _rms(x, g):
    xf = x.astype(jnp.float32)
    y = xf * lax.rsqrt(jnp.mean(xf * xf, axis=-1, keepdims=True) + EPS)
    return (y * g.astype(jnp.float32)).astype(x.dtype)


def _modulate(h, shift, scale):
    return h * (1 + scale) + shift


def _split_in(p):
    return jnp.split(p, np.cumsum(IN_SPLITS)[:-1].tolist(), axis=-1)


def _heads(t, n):
    return t.reshape(t.shape[0], t.shape[1], n, -1)


def _axial_rope(n_tokens):
    t = jnp.arange(n_tokens)
    row = (t // GRID_W).astype(jnp.float32)
    col = (t % GRID_W).astype(jnp.float32)
    quarter = MLA_ROPE_DIM // 4
    inv_freq = ROPE_THETA ** (-jnp.arange(quarter, dtype=jnp.float32) / quarter)
    ang = jnp.concatenate([row[:, None] * inv_freq, col[:, None] * inv_freq], axis=-1)
    return jnp.cos(ang), jnp.sin(ang)


def _rope(x, cos, sin):
    xf = x.astype(jnp.float32)
    half = xf.shape[-1] // 2
    x1, x2 = xf[..., :half], xf[..., half:]
    return jnp.concatenate([x1 * cos - x2 * sin, x1 * sin + x2 * cos], axis=-1).astype(x.dtype)


def _neighbourhood_attention(q, k, v, kc, vc, rpb):
    b, s, h, dh = q.shape
    rows = s // GRID_W
    kh = min(NA_WIN_H, rows)
    nqb = GRID_W // NA_QBLOCK_W
    nk = kh * NA_BAND_W
    qcol = np.arange(GRID_W).reshape(nqb, NA_QBLOCK_W)
    band_start = np.clip(np.arange(nqb) * NA_QBLOCK_W - NA_WIN_W // 2, 0, GRID_W - NA_BAND_W)
    band_cols = band_start[:, None] + np.arange(NA_BAND_W)
    win_start = np.clip(qcol - NA_WIN_W // 2, 0, GRID_W - NA_WIN_W)
    off = band_cols[:, None, :] - win_start[:, :, None]
    col_ok = (off >= 0) & (off < NA_WIN_W)
    mask = np.broadcast_to(col_ok[:, :, None, :], (nqb, NA_QBLOCK_W, kh, NA_BAND_W)).reshape(nqb, NA_QBLOCK_W, nk)
    dx_idx = np.clip(band_cols[:, None, :] - qcol[:, :, None] + NA_WIN_W - 1, 0, 2 * NA_WIN_W - 2)
    kg = k.reshape(b, rows, GRID_W, h, dh)
    vg = v.reshape(b, rows, GRID_W, h, dh)
    qg = q.reshape(b, rows, nqb, NA_QBLOCK_W, h, dh).swapaxes(0, 1)

    def row_block(args):
        r, qr = args
        r0 = jnp.clip(r - NA_WIN_H // 2, 0, rows - kh)
        kb = lax.dynamic_slice_in_dim(kg, r0, kh, axis=1)[:, :, band_cols]
        vb = lax.dynamic_slice_in_dim(vg, r0, kh, axis=1)[:, :, band_cols]
        kb = kb.transpose(0, 2, 1, 3, 4, 5).reshape(b, nqb, nk, h, dh)
        vb = vb.transpose(0, 2, 1, 3, 4, 5).reshape(b, nqb, nk, h, dh)
        dy_idx = r0 + jnp.arange(kh) - r + NA_WIN_H - 1
        bias = rpb[:, dy_idx[:, None, None, None], dx_idx[None]]
        bias = bias.transpose(2, 0, 3, 1, 4).reshape(nqb, h, NA_QBLOCK_W, nk)
        s_lat = jnp.einsum('bjqhd,bjkhd->bjhqk', qr, kb).astype(jnp.float32) * NA_SCALE + bias.astype(jnp.float32)
        s_lat = jnp.where(mask[None, :, None], s_lat, NEG_INF)
        s_ctx = jnp.einsum('bjqhd,blhd->bjhql', qr, kc).astype(jnp.float32) * NA_SCALE
        p = jax.nn.softmax(jnp.concatenate([s_lat, s_ctx], axis=-1), axis=-1).astype(v.dtype)
        return (jnp.einsum('bjhqk,bjkhd->bjqhd', p[..., :nk], vb)
                + jnp.einsum('bjhql,blhd->bjqhd', p[..., nk:], vc))

    out = lax.map(row_block, (jnp.arange(rows), qg))
    return out.swapaxes(0, 1).reshape(b, s, h * dh)


def _na_context(q, k, v):
    s = jnp.einsum('bqhd,bkhd->bhqk', q, k).astype(jnp.float32) * NA_SCALE
    p = jax.nn.softmax(s, axis=-1).astype(v.dtype)
    return jnp.einsum('bhqk,bkhd->bqhd', p, v).reshape(q.shape[0], q.shape[1], -1)


def _mla_queries(cq, g_q, w_q_b, rope):
    q = _heads(_rms(cq, g_q) @ w_q_b, MLA_HEADS)
    q_nope, q_pe = q[..., :MLA_NOPE_DIM], q[..., MLA_NOPE_DIM:]
    if rope is not None:
        q_pe = _rope(q_pe, rope[0][:, None], rope[1][:, None])
    return q_nope, q_pe


def _mla_keys(ckv, kpe, g_kv, w_kv_b, rope):
    kv = _heads(_rms(ckv, g_kv) @ w_kv_b, MLA_HEADS)
    k_nope, val = kv[..., :MLA_NOPE_DIM], kv[..., MLA_NOPE_DIM:]
    if rope is not None:
        kpe = _rope(kpe, rope[0], rope[1])
    return k_nope, kpe, val


def _mla_attend(qn, qp, kn, kp, v):
    s = jnp.einsum('bqhn,bkhn->bhqk', qn, kn) + jnp.einsum('bqhr,bkr->bhqk', qp, kp)
    p = jax.nn.softmax(s.astype(jnp.float32) * MLA_SCALE, axis=-1).astype(v.dtype)
    return jnp.einsum('bhqk,bkhv->bqhv', p, v)


def _mla_latent(qn, qp, kn, kp, v):
    b, s = qn.shape[0], qn.shape[1]
    nb = s // MLA_Q_BLOCK
    qn_b = qn.reshape(b, nb, MLA_Q_BLOCK, MLA_HEADS, MLA_NOPE_DIM).swapaxes(0, 1)
    qp_b = qp.reshape(b, nb, MLA_Q_BLOCK, MLA_HEADS, MLA_ROPE_DIM).swapaxes(0, 1)
    out = lax.map(lambda a: _mla_attend(a[0], a[1], kn, kp, v), (qn_b, qp_b))
    return out.swapaxes(0, 1).reshape(b, s, MLA_WIDTH)


def _merge(oa, ob, ga, gb, lp):
    y = jax.nn.sigmoid(ga) * (oa @ lp['w_branch_a']) + jax.nn.sigmoid(gb) * (ob @ lp['w_branch_b'])
    return y @ lp['w_out']


def _token_mixing(h, hc, lp, rope, with_ctx_out):
    qa, ka, va, cq, ckv, kpe, ga, gb = _split_in(h @ lp['w_in'])
    qac, kac, vac, cqc, ckvc, kpec, gac, gbc = _split_in(hc @ lp['w_in'])
    kac_h, vac_h = _heads(kac, NA_HEADS), _heads(vac, NA_HEADS)
    oa = _neighbourhood_attention(_heads(qa, NA_HEADS), _heads(ka, NA_HEADS), _heads(va, NA_HEADS),
                                  kac_h, vac_h, lp['na_rpb'])
    qn, qp = _mla_queries(cq, lp['mla_g_q'], lp['mla_w_q_b'], rope)
    kn, kp, vb = _mla_keys(ckv, kpe, lp['mla_g_kv'], lp['mla_w_kv_b'], rope)
    knc, kpc, vbc = _mla_keys(ckvc, kpec, lp['mla_g_kv'], lp['mla_w_kv_b'], None)
    ob = _mla_latent(qn, qp, jnp.concatenate([kn, knc], axis=1), jnp.concatenate([kp, kpc], axis=1),
                     jnp.concatenate([vb, vbc], axis=1))
    y = _merge(oa, ob, ga, gb, lp)
    if not with_ctx_out:
        return y, None
    oac = _na_context(_heads(qac, NA_HEADS), kac_h, vac_h)
    qnc, qpc = _mla_queries(cqc, lp['mla_g_q'], lp['mla_w_q_b'], None)
    obc = _mla_attend(qnc, qpc, knc, kpc, vbc).reshape(hc.shape[0], hc.shape[1], MLA_WIDTH)
    yc = _merge(oac, obc, gac, gbc, lp)
    return y, yc


def _peer(h, lp):
    b, s, d = h.shape
    t = b * s
    hf = h.reshape(t, d)
    q = (hf @ lp['peer_w_query']).reshape(t, PEER_HEADS, 2, PEER_KEY_DIM // 2)
    sc = jnp.einsum('thpk,hpnk->thpn', q, lp['peer_subkeys']).astype(jnp.float32)
    v1, i1 = lax.top_k(sc[:, :, 0], PEER_TOPK)
    v2, i2 = lax.top_k(sc[:, :, 1], PEER_TOPK)
    cand_s = (v1[..., :, None] + v2[..., None, :]).reshape(t, PEER_HEADS, PEER_TOPK * PEER_TOPK)
    cand_i = (i1[..., :, None] * PEER_N_KEYS + i2[..., None, :]).reshape(t, PEER_HEADS, PEER_TOPK * PEER_TOPK)
    top_s, pos = lax.top_k(cand_s, PEER_TOPK)
    idx = jnp.take_along_axis(cand_i, pos, axis=-1)
    gate = jax.nn.softmax(top_s, axis=-1).astype(h.dtype)
    u, v = lp['peer_u'], lp['peer_v']
    nb = t // PEER_TOKEN_BLOCK

    def blk(args):
        hb, ib, gb = args
        act = jax.nn.gelu(jnp.einsum('td,thkd->thk', hb, u[ib]), approximate=False)
        return jnp.einsum('thk,thkd->td', gb * act, v[ib])

    out = lax.map(blk, (hf.reshape(nb, PEER_TOKEN_BLOCK, d),
                        idx.reshape(nb, PEER_TOKEN_BLOCK, PEER_HEADS, PEER_TOPK),
                        gate.reshape(nb, PEER_TOKEN_BLOCK, PEER_HEADS, PEER_TOPK)))
    return out.reshape(b, s, d)


def _layer(x, xc, c, c_ctx, lp, rope, update_ctx):
    mod = (jax.nn.silu(c) @ lp['w_ada'] + lp['b_ada'])[:, None, :]
    modc = jax.nn.silu(c_ctx) @ lp['w_ada'] + lp['b_ada']
    sh1, sc1, gt1, sh2, sc2, gt2 = jnp.split(mod, 6, axis=-1)
    shc1, scc1, gtc1, shc2, scc2, gtc2 = jnp.split(modc, 6, axis=-1)
    h = _modulate(_rms(x, lp['g_pre_mix']), sh1, sc1)
    hc = _modulate(_rms(xc, lp['g_pre_mix']), shc1, scc1)
    y, yc = _token_mixing(h, hc, lp, rope, update_ctx)
    x = x + gt1 * _rms(y, lp['g_post_mix'])
    h2 = _modulate(_rms(x, lp['g_pre_ffn']), sh2, sc2)
    x = x + gt2 * _rms(_peer(h2, lp), lp['g_post_ffn'])
    if update_ctx:
        xc = xc + gtc1 * _rms(yc, lp['g_post_mix'])
        h2c = _modulate(_rms(xc, lp['g_pre_ffn']), shc2, scc2)
        xc = xc + gtc2 * _rms(_peer(h2c, lp), lp['g_post_ffn'])
    return x, xc


def setup_inputs(seed: int = 0) -> dict:
    key = jax.random.key(seed)
    ks = jax.random.split(key, 24)
    f32 = jnp.float32
    L = DEPTH

    def nrm(k, shape, scale):
        return jax.random.normal(k, shape, f32) * scale

    def gain(k, shape):
        return 1.0 + 0.1 * jax.random.normal(k, shape, f32)

    return {
        'x': nrm(ks[0], (BATCH, SEQ, D_MODEL), 1.0),
        'c': nrm(ks[1], (BATCH, D_MODEL), 1.0),
        'ctx': nrm(ks[2], (BATCH, CTX_LEN, D_MODEL), 1.0),
        'c_ctx': nrm(ks[3], (D_MODEL,), 1.0),
        'w_ada': nrm(ks[4], (L, D_MODEL, 6 * D_MODEL), 0.5 * D_MODEL ** -0.5),
        'b_ada': nrm(ks[5], (L, 6 * D_MODEL), 0.01),
        'g_pre_mix': gain(ks[6], (L, D_MODEL)),
        'g_post_mix': gain(ks[7], (L, D_MODEL)),
        'g_pre_ffn': gain(ks[8], (L, D_MODEL)),
        'g_post_ffn': gain(ks[9], (L, D_MODEL)),
        'w_in': nrm(ks[10], (L, D_MODEL, IN_COLS), D_MODEL ** -0.5),
        'na_rpb': nrm(ks[11], (L, NA_HEADS, 2 * NA_WIN_H - 1, 2 * NA_WIN_W - 1), 0.1),
        'mla_g_q': gain(ks[12], (L, MLA_Q_RANK)),
        'mla_w_q_b': nrm(ks[13], (L, MLA_Q_RANK, MLA_HEADS * (MLA_NOPE_DIM + MLA_ROPE_DIM)), MLA_Q_RANK ** -0.5),
        'mla_g_kv': gain(ks[14], (L, MLA_KV_RANK)),
        'mla_w_kv_b': nrm(ks[15], (L, MLA_KV_RANK, MLA_HEADS * (MLA_NOPE_DIM + MLA_V_DIM)), MLA_KV_RANK ** -0.5),
        'w_branch_a': nrm(ks[16], (L, NA_WIDTH, D_MODEL), NA_WIDTH ** -0.5),
        'w_branch_b': nrm(ks[17], (L, MLA_WIDTH, D_MODEL), MLA_WIDTH ** -0.5),
        'w_out': nrm(ks[18], (L, D_MODEL, D_MODEL), D_MODEL ** -0.5),
        'peer_w_query': nrm(ks[19], (L, D_MODEL, PEER_HEADS * PEER_KEY_DIM), D_MODEL ** -0.5),
        'peer_subkeys': nrm(ks[20], (L, PEER_HEADS, 2, PEER_N_KEYS, PEER_KEY_DIM // 2), (PEER_KEY_DIM // 2) ** -0.5),
        'peer_u': nrm(ks[21], (L, PEER_N_EXPERTS, D_MODEL), D_MODEL ** -0.5),
        'peer_v': nrm(ks[22], (L, PEER_N_EXPERTS, D_MODEL), (PEER_HEADS * PEER_TOPK) ** -0.5),
    }


def reference(x, c, ctx, c_ctx, w_ada, b_ada, g_pre_mix, g_post_mix, g_pre_ffn, g_post_ffn, w_in, na_rpb,
              mla_g_q, mla_w_q_b, mla_g_kv, mla_w_kv_b, w_branch_a, w_branch_b, w_out,
              peer_w_query, peer_subkeys, peer_u, peer_v):
    rope = _axial_rope(x.shape[1])
    xc = ctx
    for l in range(DEPTH):
        lp = {
            'w_ada': w_ada[l], 'b_ada': b_ada[l],
            'g_pre_mix': g_pre_mix[l], 'g_post_mix': g_post_mix[l],
            'g_pre_ffn': g_pre_ffn[l], 'g_post_ffn': g_post_ffn[l],
            'w_in': w_in[l], 'na_rpb': na_rpb[l],
            'mla_g_q': mla_g_q[l], 'mla_w_q_b': mla_w_q_b[l],
            'mla_g_kv': mla_g_kv[l], 'mla_w_kv_b': mla_w_kv_b[l],
            'w_branch_a': w_branch_a[l], 'w_branch_b': w_branch_b[l], 'w_out': w_out[l],
            'peer_w_query': peer_w_query[l], 'peer_subkeys': peer_subkeys[l],
            'peer_u': peer_u[l], 'peer_v': peer_v[l],
        }
        x, xc = _layer(x, xc, c, c_ctx, lp, rope, l < DEPTH - 1)
    return x
```

```python
import functools
import math

import numpy as np
import jax
import jax.numpy as jnp
from jax import lax
from jax.experimental import pallas as pl
from jax.experimental.pallas import tpu as pltpu

F32 = jnp.float32
BF16 = jnp.bfloat16

EPS = 1e-6
NEG_INF = -1e30
GRID_W = 64

NA_HEADS = 16
NA_HEAD_DIM = 64
NA_WIDTH = NA_HEADS * NA_HEAD_DIM
NA_WIN_H = 8
NA_WIN_W = 16
NA_SCALE = NA_HEAD_DIM ** -0.5
NA_PAIRS = NA_HEADS // 2
NA_GROUP_ROWS = 8
NA_KEY_ROWS = 16

MLA_HEADS = 8
MLA_NOPE = 128
MLA_ROPE = 64
MLA_V = 128
MLA_WIDTH = MLA_HEADS * MLA_V
MLA_Q_RANK = 768
MLA_KV_RANK = 512
MLA_SCALE = (MLA_NOPE + MLA_ROPE) ** -0.5
ROPE_THETA = 10000.0

PEER_HEADS = 8
PEER_KEYS = 128
PEER_TOPK = 16
PEER_KEY_DIM = 256

LANE = 128
VMEM_LIMIT = 56 * 1024 * 1024


def _cparams(sem):
    return pltpu.CompilerParams(dimension_semantics=sem, vmem_limit_bytes=VMEM_LIMIT)


def _ada_kernel(c_ref, w_ref, b_ref, o_ref):
    c = c_ref[...]
    a = c * jax.nn.sigmoid(c)
    o_ref[...] = jnp.dot(a, w_ref[...], preferred_element_type=F32) + b_ref[...]


def _ada(cc, w, b):
    m, k = cc.shape
    n = w.shape[1]
    tn = 1024
    return pl.pallas_call(
        _ada_kernel,
        grid=(n // tn,),
        in_specs=[pl.BlockSpec((m, k), lambda j: (0, 0)),
                  pl.BlockSpec((k, tn), lambda j: (0, j)),
                  pl.BlockSpec((1, tn), lambda j: (0, j))],
        out_specs=pl.BlockSpec((m, tn), lambda j: (0, j)),
        out_shape=jax.ShapeDtypeStruct((m, n), F32),
        compiler_params=_cparams(("parallel",)),
        name="ada",
    )(cc, w, b)


def _rms_rows(x, g):
    return x * lax.rsqrt(jnp.mean(x * x, axis=-1, keepdims=True) + EPS) * g


def _in_proj_kernel(x_ref, g_ref, sh_ref, sc_ref, w_ref, o_ref, h_ref, *, chunk):
    @pl.when(pl.program_id(1) == 0)
    def _():
        g = g_ref[...]
        sh = sh_ref[...]
        sc1 = 1.0 + sc_ref[...]

        def body(r, carry):
            rows = pl.ds(pl.multiple_of(r * chunk, chunk), chunk)
            h_ref[rows, :] = (_rms_rows(x_ref[rows, :], g) * sc1 + sh).astype(BF16)
            return carry

        lax.fori_loop(0, x_ref.shape[0] // chunk, body, 0)

    o_ref[...] = jnp.dot(h_ref[...], w_ref[...], preferred_element_type=F32).astype(BF16)


def _in_proj(x2d, g, shift, scale, w, tiles_per_mod, tm, tn):
    m, k = x2d.shape
    n = w.shape[1]
    return pl.pallas_call(
        functools.partial(_in_proj_kernel, chunk=128),
        grid=(m // tm, n // tn),
        in_specs=[pl.BlockSpec((tm, k), lambda i, j: (i, 0)),
                  pl.BlockSpec((1, k), lambda i, j: (0, 0)),
                  pl.BlockSpec((None, 1, k), lambda i, j: (i // tiles_per_mod, 0, 0)),
                  pl.BlockSpec((None, 1, k), lambda i, j: (i // tiles_per_mod, 0, 0)),
                  pl.BlockSpec((k, tn), lambda i, j: (0, j))],
        out_specs=pl.BlockSpec((tm, tn), lambda i, j: (i, j)),
        out_shape=jax.ShapeDtypeStruct((m, n), BF16),
        scratch_shapes=[pltpu.VMEM((tm, k), BF16)],
        compiler_params=_cparams(("parallel", "arbitrary")),
        name="in_proj",
    )(x2d, g, shift, scale, w)


def _mla_q_kernel(cq_ref, g_ref, w_ref, cos_ref, sin_ref, qn_ref, qp_ref):
    h = _rms_rows(cq_ref[...].astype(F32), g_ref[...]).astype(BF16)
    q = jnp.dot(h, w_ref[...], preferred_element_type=F32)
    nw = MLA_HEADS * MLA_NOPE
    rw = MLA_HEADS * MLA_ROPE
    qn_ref[...] = (q[:, :nw] * MLA_SCALE).astype(BF16)
    roped = q[:, nw:nw + rw] * cos_ref[...] + q[:, nw + rw:] * sin_ref[...]
    qp_ref[...] = (roped * MLA_SCALE).astype(BF16)


def _mla_q(p, col_block, g, w, cos_t, sin_t, seq, tm):
    t = p.shape[0]
    per = seq // tm
    nw = MLA_HEADS * MLA_NOPE
    rw = MLA_HEADS * MLA_ROPE
    return pl.pallas_call(
        _mla_q_kernel,
        grid=(t // tm,),
        in_specs=[pl.BlockSpec((tm, MLA_Q_RANK), lambda i: (i, col_block)),
                  pl.BlockSpec((1, MLA_Q_RANK), lambda i: (0, 0)),
                  pl.BlockSpec(w.shape, lambda i: (0, 0)),
                  pl.BlockSpec((tm, rw), lambda i: (i % per, 0)),
                  pl.BlockSpec((tm, rw), lambda i: (i % per, 0))],
        out_specs=[pl.BlockSpec((tm, nw), lambda i: (i, 0)),
                   pl.BlockSpec((tm, rw), lambda i: (i, 0))],
        out_shape=[jax.ShapeDtypeStruct((t, nw), BF16), jax.ShapeDtypeStruct((t, rw), BF16)],
        compiler_params=_cparams(("parallel",)),
        name="mla_q",
    )(p, g, w, cos_t, sin_t)


def _mla_kv_kernel(ckv_ref, kpe_ref, g_ref, w_ref, tab_ref, kn_ref, v_ref, kp_ref):
    h = _rms_rows(ckv_ref[...].astype(F32), g_ref[...]).astype(BF16)
    kv = jnp.dot(h, w_ref[...], preferred_element_type=F32)
    nw = MLA_HEADS * MLA_NOPE
    kn_ref[...] = kv[:, :nw].astype(BF16)
    v_ref[...] = kv[:, nw:].astype(BF16)
    t = kpe_ref[...].astype(F32) * tab_ref[...]
    kp_ref[...] = (t + pltpu.roll(t, MLA_ROPE, 1)).astype(BF16)


def _mla_kv(p, ckv_block, kpe_block, g, w, tab, seq, tm):
    t = p.shape[0]
    per = seq // tm
    nw = MLA_HEADS * MLA_NOPE
    vw = MLA_HEADS * MLA_V
    return pl.pallas_call(
        _mla_kv_kernel,
        grid=(t // tm,),
        in_specs=[pl.BlockSpec((tm, MLA_KV_RANK), lambda i: (i, ckv_block)),
                  pl.BlockSpec((tm, LANE), lambda i: (i, kpe_block)),
                  pl.BlockSpec((1, MLA_KV_RANK), lambda i: (0, 0)),
                  pl.BlockSpec(w.shape, lambda i: (0, 0)),
                  pl.BlockSpec((tm, LANE), lambda i: (i % per, 0))],
        out_specs=[pl.BlockSpec((tm, nw), lambda i: (i, 0)),
                   pl.BlockSpec((tm, vw), lambda i: (i, 0)),
                   pl.BlockSpec((tm, LANE), lambda i: (i, 0))],
        out_shape=[jax.ShapeDtypeStruct((t, nw), BF16), jax.ShapeDtypeStruct((t, vw), BF16),
                   jax.ShapeDtypeStruct((t, LANE), BF16)],
        compiler_params=_cparams(("parallel",)),
        name="mla_kv",
    )(p, p, g, w, tab)


def _na_group_plan(rows):
    kh = min(NA_WIN_H, rows)
    plan = []
    for g in range(rows // NA_GROUP_ROWS):
        start = int(np.clip(NA_GROUP_ROWS * g - NA_WIN_H // 2, 0, rows - NA_KEY_ROWS))
        per_row = []
        for j in range(NA_GROUP_ROWS):
            r = NA_GROUP_ROWS * g + j
            r0 = int(np.clip(r - NA_WIN_H // 2, 0, rows - kh))
            assert start <= r0 and r0 + kh <= start + NA_KEY_ROWS
            per_row.append((r, r0, r0 + kh - 1))
        plan.append((start, per_row))
    return plan


def _na_kernel(q_ref, k_ref, v_ref, kc_ref, vc_ref, bias_ref, o_ref, s_ref, sc_ref, p_ref, pc_ref, l_ref,
               *, rows):
    gq = NA_GROUP_ROWS * GRID_W
    gk = NA_KEY_ROWS * GRID_W
    lane_lo = lax.broadcasted_iota(jnp.int32, (1, LANE), 1) < NA_HEAD_DIM
    half_lo = lax.broadcasted_iota(jnp.int32, (GRID_W, LANE), 1) < GRID_W
    kc = kc_ref[...]
    vc = vc_ref[...]
    contract_last = (((1,), (1,)), ((), ()))
    for g, (start, per_row) in enumerate(_na_group_plan(rows)):
        q2 = q_ref[g * gq:(g + 1) * gq, :] * NA_SCALE
        kwin = k_ref[start * GRID_W:start * GRID_W + gk, :]
        vwin = v_ref[start * GRID_W:start * GRID_W + gk, :]
        outs = []
        for half in range(2):
            keep = lane_lo if half == 0 else jnp.logical_not(lane_lo)
            qh = jnp.where(keep, q2, jnp.zeros_like(q2))
            s_ref[...] = lax.dot_general(qh, kwin, contract_last, preferred_element_type=F32)
            sc_ref[...] = lax.dot_general(qh, kc, contract_last, preferred_element_type=F32)
            for j, (r, lo, hi) in enumerate(per_row):
                rs = slice(j * GRID_W, (j + 1) * GRID_W)
                blocks = []
                for m in range(NA_KEY_ROWS // 2):
                    ra, rb = start + 2 * m, start + 2 * m + 1
                    va, vb = lo <= ra <= hi, lo <= rb <= hi
                    if not (va or vb):
                        blocks.append(None)
                        continue
                    dy_a = ra - r + NA_WIN_H - 1
                    blk = s_ref[rs, m * LANE:(m + 1) * LANE] + bias_ref[half, dy_a + 1]
                    if not va:
                        blk = jnp.where(half_lo, NEG_INF, blk)
                    if not vb:
                        blk = jnp.where(half_lo, blk, NEG_INF)
                    blocks.append(blk)
                sctx = sc_ref[rs, :]
                mx = jnp.max(sctx, axis=-1, keepdims=True)
                for blk in blocks:
                    if blk is not None:
                        mx = jnp.maximum(mx, jnp.max(blk, axis=-1, keepdims=True))
                pctx = jnp.exp(sctx - mx)
                den = jnp.sum(pctx, axis=-1, keepdims=True)
                pc_ref[rs, :] = pctx.astype(BF16)
                for m, blk in enumerate(blocks):
                    cs = slice(m * LANE, (m + 1) * LANE)
                    if blk is None:
                        p_ref[rs, cs] = jnp.zeros((GRID_W, LANE), BF16)
                    else:
                        e = jnp.exp(blk - mx)
                        den = den + jnp.sum(e, axis=-1, keepdims=True)
                        p_ref[rs, cs] = e.astype(BF16)
                l_ref[rs, :] = jnp.broadcast_to(den, (GRID_W, LANE))
            o = (jnp.dot(p_ref[...], vwin, preferred_element_type=F32)
                 + jnp.dot(pc_ref[...], vc, preferred_element_type=F32))
            outs.append(o / l_ref[...])
        o_ref[g * gq:(g + 1) * gq, :] = jnp.where(lane_lo, outs[0], outs[1]).astype(BF16)


def _na_attn(p, pc, bias, batch, seq, q_blk, k_blk, v_blk, kc_blk, vc_blk):
    rows = seq // GRID_W
    ctx_len = pc.shape[0] // batch
    gq = NA_GROUP_ROWS * GRID_W
    gk = NA_KEY_ROWS * GRID_W
    return pl.pallas_call(
        functools.partial(_na_kernel, rows=rows),
        grid=(batch, NA_PAIRS),
        in_specs=[pl.BlockSpec((seq, LANE), lambda b, h: (b, q_blk + h)),
                  pl.BlockSpec((seq, LANE), lambda b, h: (b, k_blk + h)),
                  pl.BlockSpec((seq, LANE), lambda b, h: (b, v_blk + h)),
                  pl.BlockSpec((ctx_len, LANE), lambda b, h: (b, kc_blk + h)),
                  pl.BlockSpec((ctx_len, LANE), lambda b, h: (b, vc_blk + h)),
                  pl.BlockSpec((2,) + bias.shape[1:], lambda b, h: (h, 0, 0, 0))],
        out_specs=pl.BlockSpec((seq, LANE), lambda b, h: (b, h)),
        out_shape=jax.ShapeDtypeStruct((batch * seq, NA_WIDTH), BF16),
        scratch_shapes=[pltpu.VMEM((gq, gk), F32), pltpu.VMEM((gq, ctx_len), F32),
                        pltpu.VMEM((gq, gk), BF16), pltpu.VMEM((gq, ctx_len), BF16),
                        pltpu.VMEM((gq, LANE), F32)],
        compiler_params=_cparams(("parallel", "parallel")),
        name="na_attn",
    )(p, p, p, pc, pc, bias)


def _mla_attn_kernel(qn_ref, qp_ref, kn_ref, kp_ref, v_ref, knc_ref, kpc_ref, vc_ref, o_ref):
    lane_lo = lax.broadcasted_iota(jnp.int32, (1, LANE), 1) < MLA_ROPE
    contract_last = (((1,), (1,)), ((), ()))
    kp = kp_ref[...]
    kpc = kpc_ref[...]
    for h in range(MLA_HEADS):
        cs = slice(h * LANE, (h + 1) * LANE)
        qn = qn_ref[:, cs]
        qp2 = qp_ref[:, (h // 2) * LANE:(h // 2 + 1) * LANE]
        keep = lane_lo if h % 2 == 0 else jnp.logical_not(lane_lo)
        qp = jnp.where(keep, qp2, jnp.zeros_like(qp2))
        s = (lax.dot_general(qn, kn_ref[:, cs], contract_last, preferred_element_type=F32)
             + lax.dot_general(qp, kp, contract_last, preferred_element_type=F32))
        sc = (lax.dot_general(qn, knc_ref[:, cs], contract_last, preferred_element_type=F32)
              + lax.dot_general(qp, kpc, contract_last, preferred_element_type=F32))
        mx = jnp.maximum(jnp.max(s, axis=-1, keepdims=True), jnp.max(sc, axis=-1, keepdims=True))
        e = jnp.exp(s - mx)
        ec = jnp.exp(sc - mx)
        den = jnp.sum(e, axis=-1, keepdims=True) + jnp.sum(ec, axis=-1, keepdims=True)
        o = (jnp.dot(e.astype(BF16), v_ref[:, cs], preferred_element_type=F32)
             + jnp.dot(ec.astype(BF16), vc_ref[:, cs], preferred_element_type=F32))
        o_ref[:, cs] = (o / den).astype(BF16)


def _mla_attn(qn, qp, kn, kp, v, knc, kpc, vc, batch, seq, tq):
    ctx_len = knc.shape[0] // batch
    per = seq // tq
    w = MLA_WIDTH
    return pl.pallas_call(
        _mla_attn_kernel,
        grid=(batch, per),
        in_specs=[pl.BlockSpec((tq, w), lambda b, i: (b * per + i, 0)),
                  pl.BlockSpec((tq, qp.shape[1]), lambda b, i: (b * per + i, 0)),
                  pl.BlockSpec((seq, w), lambda b, i: (b, 0)),
                  pl.BlockSpec((seq, LANE), lambda b, i: (b, 0)),
                  pl.BlockSpec((seq, w), lambda b, i: (b, 0)),
                  pl.BlockSpec((ctx_len, w), lambda b, i: (b, 0)),
                  pl.BlockSpec((ctx_len, LANE), lambda b, i: (b, 0)),
                  pl.BlockSpec((ctx_len, w), lambda b, i: (b, 0))],
        out_specs=pl.BlockSpec((tq, w), lambda b, i: (b * per + i, 0)),
        out_shape=jax.ShapeDtypeStruct((batch * seq, w), BF16),
        compiler_params=_cparams(("parallel", "parallel")),
        name="mla_attn",
    )(qn, qp, kn, kp, v, knc, kpc, vc)


def _merge_kernel(oa_ref, ob_ref, ga_ref, gb_ref, x_ref, wa_ref, wb_ref, wo_ref,
                  gpost_ref, gt1_ref, gpre_ref, sh2_ref, sc2_ref, x1_ref, h2_ref):
    ya = jnp.dot(oa_ref[...], wa_ref[...], preferred_element_type=F32)
    yb = jnp.dot(ob_ref[...], wb_ref[...], preferred_element_type=F32)
    y = (jax.nn.sigmoid(ga_ref[...].astype(F32)) * ya + jax.nn.sigmoid(gb_ref[...].astype(F32)) * yb)
    z = jnp.dot(y.astype(BF16), wo_ref[...], preferred_element_type=F32)
    x1 = x_ref[...] + gt1_ref[...] * _rms_rows(z, gpost_ref[...])
    x1_ref[...] = x1
    h2_ref[...] = (_rms_rows(x1, gpre_ref[...]) * (1.0 + sc2_ref[...]) + sh2_ref[...]).astype(BF16)


def _merge(oa, ob, p, ga_blk, gb_blk, x2d, wa, wb, wo, gpost, gt1, gpre, sh2, sc2, seq, tm):
    t, d = x2d.shape
    per = seq // tm
    const = lambda i: (0, 0)
    mod = lambda i: (i // per, 0, 0)
    return pl.pallas_call(
        _merge_kernel,
        grid=(t // tm,),
        in_specs=[pl.BlockSpec((tm, oa.shape[1]), lambda i: (i, 0)),
                  pl.BlockSpec((tm, ob.shape[1]), lambda i: (i, 0)),
                  pl.BlockSpec((tm, d), lambda i: (i, ga_blk)),
                  pl.BlockSpec((tm, d), lambda i: (i, gb_blk)),
                  pl.BlockSpec((tm, d), lambda i: (i, 0)),
                  pl.BlockSpec(wa.shape, const, pipeline_mode=pl.Buffered(1)),
                  pl.BlockSpec(wb.shape, const, pipeline_mode=pl.Buffered(1)),
                  pl.BlockSpec(wo.shape, const, pipeline_mode=pl.Buffered(1)),
                  pl.BlockSpec((1, d), const),
                  pl.BlockSpec((None, 1, d), mod),
                  pl.BlockSpec((1, d), const),
                  pl.BlockSpec((None, 1, d), mod),
                  pl.BlockSpec((None, 1, d), mod)],
        out_specs=[pl.BlockSpec((tm, d), lambda i: (i, 0)),
                   pl.BlockSpec((tm, d), lambda i: (i, 0))],
        out_shape=[jax.ShapeDtypeStruct((t, d), F32), jax.ShapeDtypeStruct((t, d), BF16)],
        compiler_params=_cparams(("parallel",)),
        name="merge",
    )(oa, ob, p, p, x2d, wa, wb, wo, gpost, gt1, gpre, sh2, sc2)


def _peer_scores_kernel(h_ref, wq_ref, sub_ref, o_ref):
    contract_last = (((1,), (1,)), ((), ()))
    qt = lax.dot_general(wq_ref[...], h_ref[...], contract_last, preferred_element_type=F32).astype(BF16)
    half = PEER_KEY_DIM // 2
    for hp in range(2 * PEER_HEADS):
        o_ref[hp] = jnp.dot(sub_ref[hp], qt[hp * half:(hp + 1) * half, :], preferred_element_type=F32)


def _peer_scores(h2, wq_t, sub, tm):
    t, d = h2.shape
    n_hp = sub.shape[0]
    return pl.pallas_call(
        _peer_scores_kernel,
        grid=(t // tm,),
        in_specs=[pl.BlockSpec((tm, d), lambda i: (i, 0)),
                  pl.BlockSpec(wq_t.shape, lambda i: (0, 0)),
                  pl.BlockSpec(sub.shape, lambda i: (0, 0, 0))],
        out_specs=pl.BlockSpec((n_hp, PEER_KEYS, tm), lambda i: (0, 0, i)),
        out_shape=jax.ShapeDtypeStruct((n_hp, PEER_KEYS, t), F32),
        compiler_params=_cparams(("parallel",)),
        name="peer_scores",
    )(h2, wq_t, sub)


_TOPN = PEER_TOPK + 1
_VROWS = 24


def _sorted_top(s, row_id):
    out = jnp.full((_VROWS, s.shape[1]), -jnp.inf, F32)
    for it in range(_TOPN):
        m = jnp.max(s, axis=0, keepdims=True)
        out = jnp.where(row_id == it, m, out)
        if it + 1 < _TOPN:
            s = jnp.where(s >= m, -jnp.inf, s)
    return out


def _peer_topk_kernel(sc_ref, a_ref, d_ref, b_ref):
    lanes = sc_ref.shape[-1]
    row24 = lax.broadcasted_iota(jnp.int32, (_VROWS, lanes), 0)
    row8 = lax.broadcasted_iota(jnp.int32, (8, lanes), 0)
    for h in range(PEER_HEADS):
        s1 = sc_ref[2 * h]
        s2 = sc_ref[2 * h + 1]
        v1 = _sorted_top(s1, row24)
        v2 = _sorted_top(s2, row24)
        cands = [v1[0:1] + v2]
        for j in range(1, 8):
            cands.append(jnp.where(row8 < _TOPN // (j + 1), v1[j:j + 1] + v2[0:8], -jnp.inf))
        cands.append(jnp.where(row24 >= 8, v1 + v2[0:1], -jnp.inf))
        cand = jnp.concatenate(cands, axis=0)
        work = cand
        for it in range(_TOPN):
            m = jnp.max(work, axis=0, keepdims=True)
            if it == PEER_TOPK - 1:
                p16 = m
            if it + 1 < _TOPN:
                work = jnp.where(work >= m, -jnp.inf, work)
        tau = 0.5 * (p16 + m)
        m1 = v1[0:1]
        m2 = v2[0:1]
        z = jnp.sum(jnp.where(cand > tau, jnp.exp(cand - (m1 + m2)), 0.0), axis=0, keepdims=True)
        a_ref[h] = jnp.exp(s1 - m1)
        d_ref[h] = tau - s1
        b_ref[h] = jnp.exp(s2 - m2) / z


def _peer_topk(sc, tl):
    n_hp, nk, t = sc.shape
    out = jax.ShapeDtypeStruct((PEER_HEADS, nk, t), F32)
    spec = pl.BlockSpec((PEER_HEADS, nk, tl), lambda i: (0, 0, i))
    return pl.pallas_call(
        _peer_topk_kernel,
        grid=(t // tl,),
        in_specs=[pl.BlockSpec((n_hp, nk, tl), lambda i: (0, 0, i))],
        out_specs=[spec, spec, spec],
        out_shape=[out, out, out],
        compiler_params=_cparams(("parallel",)),
        name="peer_topk",
    )(sc)


def _gelu_exact(x):
    return 0.5 * x * (1.0 + lax.erf(x * math.sqrt(0.5)))


def _peer_dense_kernel(h_ref, u_ref, vt_ref, a_ref, d_ref, b_ref, s2_ref, o_ref, at_ref):
    e = pl.program_id(1)
    te = u_ref.shape[0]
    tm = h_ref.shape[0]
    contract_last = (((1,), (1,)), ((), ()))

    @pl.when(e == 0)
    def _():
        o_ref[...] = jnp.zeros_like(o_ref)

    st = lax.dot_general(u_ref[...], h_ref[...], contract_last, preferred_element_type=F32)
    for k in range(te // PEER_KEYS):
        i1 = e * (te // PEER_KEYS) + k
        act = _gelu_exact(st[k * PEER_KEYS:(k + 1) * PEER_KEYS, :])
        gate = jnp.zeros((PEER_KEYS, tm), F32)
        for h in range(PEER_HEADS):
            a_row = a_ref[h, pl.ds(i1, 1), :]
            d_row = d_ref[h, pl.ds(i1, 1), :]
            gate = gate + jnp.where(s2_ref[h] > d_row, b_ref[h], 0.0) * a_row
        at_ref[k * PEER_KEYS:(k + 1) * PEER_KEYS, :] = (gate * act).astype(BF16)
    o_ref[...] += jnp.dot(vt_ref[...], at_ref[...], preferred_element_type=F32)


def _peer_dense(h2, u, vt, a, dd, b, sc4, tm, te):
    t, d = h2.shape
    n_exp = u.shape[0]
    stat = pl.BlockSpec((PEER_HEADS, PEER_KEYS, tm), lambda i, e: (0, 0, i))
    return pl.pallas_call(
        _peer_dense_kernel,
        grid=(t // tm, n_exp // te),
        in_specs=[pl.BlockSpec((tm, d), lambda i, e: (i, 0)),
                  pl.BlockSpec((te, d), lambda i, e: (e, 0)),
                  pl.BlockSpec((d, te), lambda i, e: (0, e)),
                  stat, stat, stat,
                  pl.BlockSpec((PEER_HEADS, None, PEER_KEYS, tm), lambda i, e: (0, 1, 0, i))],
        out_specs=pl.BlockSpec((d, tm), lambda i, e: (0, i)),
        out_shape=jax.ShapeDtypeStruct((d, t), F32),
        scratch_shapes=[pltpu.VMEM((te, tm), BF16)],
        compiler_params=_cparams(("parallel", "arbitrary")),
        name="peer_dense",
    )(h2, u, vt, a, dd, b, sc4)


def _peer_out_kernel(pt_ref, x1_ref, g_ref, gt2_ref, o_ref):
    y = pt_ref[...].T
    o_ref[...] = x1_ref[...] + gt2_ref[...] * _rms_rows(y, g_ref[...])


def _peer_out(peer_t, x1, g, gt2, seq, tm):
    t, d = x1.shape
    per = seq // tm
    return pl.pallas_call(
        _peer_out_kernel,
        grid=(t // tm,),
        in_specs=[pl.BlockSpec((d, tm), lambda i: (0, i)),
                  pl.BlockSpec((tm, d), lambda i: (i, 0)),
                  pl.BlockSpec((1, d), lambda i: (0, 0)),
                  pl.BlockSpec((None, 1, d), lambda i: (i // per, 0, 0))],
        out_specs=pl.BlockSpec((tm, d), lambda i: (i, 0)),
        out_shape=jax.ShapeDtypeStruct((t, d), F32),
        compiler_params=_cparams(("parallel",)),
        name="peer_out",
    )(peer_t, x1, g, gt2)


def _rope_tables(seq):
    t = jnp.arange(seq)
    row = (t // GRID_W).astype(F32)
    col = (t % GRID_W).astype(F32)
    quarter = MLA_ROPE // 4
    inv_freq = ROPE_THETA ** (-jnp.arange(quarter, dtype=F32) / quarter)
    ang = jnp.concatenate([row[:, None] * inv_freq, col[:, None] * inv_freq], axis=-1)
    cos, sin = jnp.cos(ang), jnp.sin(ang)
    return jnp.concatenate([cos, cos], axis=-1), jnp.concatenate([-sin, sin], axis=-1)


def _swap_halves(w, width):
    lead = w.shape[0]
    g = w.reshape(lead, -1, 2, width // 2)
    return g[:, :, ::-1, :].reshape(lead, -1)


def _na_bias_table(rpb):
    cols = np.arange(GRID_W)
    win_start = np.clip(cols - NA_WIN_W // 2, 0, GRID_W - NA_WIN_W)
    off = cols[None, :] - win_start[:, None]
    col_ok = (off >= 0) & (off < NA_WIN_W)
    dx = np.clip(cols[None, :] - cols[:, None] + NA_WIN_W - 1, 0, 2 * NA_WIN_W - 2)
    t = jnp.where(col_ok[None, None], rpb[:, :, dx], NEG_INF)
    neg = jnp.full((rpb.shape[0], 1, GRID_W, GRID_W), NEG_INF, F32)
    t = jnp.concatenate([neg, t, neg], axis=1)
    return jnp.concatenate([t[:, :-1], t[:, 1:]], axis=-1)


def kernel(x, c, ctx, c_ctx, w_ada, b_ada, g_pre_mix, g_post_mix, g_pre_ffn, g_post_ffn, w_in, na_rpb,
           mla_g_q, mla_w_q_b, mla_g_kv, mla_w_kv_b, w_branch_a, w_branch_b, w_out,
           peer_w_query, peer_subkeys, peer_u, peer_v):
    batch, seq, d = x.shape
    ctx_len = ctx.shape[1]
    assert w_ada.shape[0] == 1 and seq % (GRID_W * NA_GROUP_ROWS) == 0 and seq // GRID_W >= NA_KEY_ROWS
    t = batch * seq

    n_mod = batch + 1
    pad = (-n_mod) % 8
    cc = jnp.concatenate([c, c_ctx[None], jnp.zeros((pad, d), F32)], axis=0)
    mod = _ada(cc, w_ada[0], b_ada[0][None])
    sh1, sc1, gt1, sh2, sc2, gt2 = [m[:, None, :] for m in jnp.split(mod[:n_mod], 6, axis=-1)]

    w = w_in[0]
    o = np.cumsum((0, NA_WIDTH, NA_WIDTH, NA_WIDTH, MLA_Q_RANK, MLA_KV_RANK, MLA_ROPE, d, d))
    qa, ka, va, cq, ckv, kpe, ga, gb = [w[:, o[i]:o[i + 1]] for i in range(8)]
    kpe2 = jnp.concatenate([kpe, _swap_halves(kpe, MLA_ROPE)], axis=1)
    zpad = lambda n: jnp.zeros((d, n), F32)
    w_lat = jnp.concatenate([ga, gb, qa, ka, va, ckv, cq, kpe2, zpad(LANE)], axis=1).astype(BF16)
    w_ctx = jnp.concatenate([ka, va, ckv, kpe2, zpad(3 * LANE)], axis=1).astype(BF16)
    lat = dict(ga=0, gb=1, q=(2 * d) // LANE, k=(2 * d + NA_WIDTH) // LANE, v=(2 * d + 2 * NA_WIDTH) // LANE,
               ckv=(2 * d + 3 * NA_WIDTH) // MLA_KV_RANK,
               cq=(2 * d + 3 * NA_WIDTH + MLA_KV_RANK) // MLA_Q_RANK,
               kpe=(2 * d + 3 * NA_WIDTH + MLA_KV_RANK + MLA_Q_RANK) // LANE)
    assert (2 * d + 3 * NA_WIDTH + MLA_KV_RANK) % MLA_Q_RANK == 0
    cx = dict(k=0, v=NA_WIDTH // LANE, ckv=(2 * NA_WIDTH) // MLA_KV_RANK, kpe=(2 * NA_WIDTH + MLA_KV_RANK) // LANE)

    g_pre = g_pre_mix[0][None]
    p_lat = _in_proj(x.reshape(t, d), g_pre, sh1[:batch], sc1[:batch], w_lat, seq // 1024, 1024, 512)
    p_ctx = _in_proj(ctx.reshape(batch * ctx_len, d), g_pre, sh1[batch:], sc1[batch:], w_ctx,
                     batch * ctx_len // 1024 if batch * ctx_len >= 1024 else 1,
                     min(1024, batch * ctx_len), 512)

    c64, s64 = _rope_tables(seq)
    wq = mla_w_q_b[0].reshape(MLA_Q_RANK, MLA_HEADS, MLA_NOPE + MLA_ROPE)
    wq_pe = wq[:, :, MLA_NOPE:].reshape(MLA_Q_RANK, -1)
    wq2 = jnp.concatenate([wq[:, :, :MLA_NOPE].reshape(MLA_Q_RANK, -1), wq_pe,
                           _swap_halves(wq_pe, MLA_ROPE)], axis=1).astype(BF16)
    wkv = mla_w_kv_b[0].reshape(MLA_KV_RANK, MLA_HEADS, MLA_NOPE + MLA_V)
    wkv2 = jnp.concatenate([wkv[:, :, :MLA_NOPE].reshape(MLA_KV_RANK, -1),
                            wkv[:, :, MLA_NOPE:].reshape(MLA_KV_RANK, -1)], axis=1).astype(BF16)
    qn, qp = _mla_q(p_lat, lat["cq"], mla_g_q, wq2, jnp.tile(c64, (1, MLA_HEADS)),
                    jnp.tile(s64, (1, MLA_HEADS)), seq, 512)
    kn, vv, kp = _mla_kv(p_lat, lat["ckv"], lat["kpe"], mla_g_kv, wkv2,
                         jnp.concatenate([c64, s64], axis=1), seq, 512)
    tab_ctx = jnp.concatenate([jnp.ones((ctx_len, MLA_ROPE), F32), jnp.zeros((ctx_len, MLA_ROPE), F32)], axis=1)
    knc, vvc, kpc = _mla_kv(p_ctx, cx["ckv"], cx["kpe"], mla_g_kv, wkv2, tab_ctx, ctx_len, ctx_len)

    oa = _na_attn(p_lat, p_ctx, _na_bias_table(na_rpb[0]), batch, seq,
                  lat["q"], lat["k"], lat["v"], cx["k"], cx["v"])
    ob = _mla_attn(qn, qp, kn, kp, vv, knc, kpc, vvc, batch, seq, 256)
    x1, h2 = _merge(oa, ob, p_lat, lat["ga"], lat["gb"], x.reshape(t, d),
                    w_branch_a[0].astype(BF16), w_branch_b[0].astype(BF16), w_out[0].astype(BF16),
                    g_post_mix[0][None], gt1[:batch], g_pre_ffn[0][None], sh2[:batch], sc2[:batch], seq, 256)

    sub = peer_subkeys[0].reshape(2 * PEER_HEADS, PEER_KEYS, PEER_KEY_DIM // 2).astype(BF16)
    sc = _peer_scores(h2, peer_w_query[0].T.astype(BF16), sub, 512)
    a, dd, b = _peer_topk(sc, 256)
    sc4 = sc.reshape(PEER_HEADS, 2, PEER_KEYS, t)
    peer_t = _peer_dense(h2, peer_u[0].astype(BF16), peer_v[0].T.astype(BF16), a, dd, b, sc4, 512, 512)
    out = _peer_out(peer_t, x1, g_post_ffn[0][None], gt2[:batch], seq, 256)
    return out.reshape(batch, seq, d)
```

```python
import functools
import math

import numpy as np
import jax
import jax.numpy as jnp
from jax import lax
from jax.experimental import pallas as pl
from jax.experimental.pallas import tpu as pltpu

F32 = jnp.float32
BF16 = jnp.bfloat16

EPS = 1e-6
NEG_INF = -1e30
GRID_W = 64

NA_HEADS = 16
NA_HEAD_DIM = 64
NA_WIDTH = NA_HEADS * NA_HEAD_DIM
NA_WIN_H = 8
NA_WIN_W = 16
NA_SCALE = NA_HEAD_DIM ** -0.5
NA_PAIRS = NA_HEADS // 2
NA_GROUP_ROWS = 8
NA_KEY_ROWS = 16

MLA_HEADS = 8
MLA_NOPE = 128
MLA_ROPE = 64
MLA_V = 128
MLA_WIDTH = MLA_HEADS * MLA_V
MLA_Q_RANK = 768
MLA_KV_RANK = 512
MLA_SCALE = (MLA_NOPE + MLA_ROPE) ** -0.5
ROPE_THETA = 10000.0

PEER_HEADS = 8
PEER_KEYS = 128
PEER_TOPK = 16
PEER_KEY_DIM = 256

LANE = 128
VMEM_LIMIT = 56 * 1024 * 1024


def _cparams(sem):
    return pltpu.CompilerParams(dimension_semantics=sem, vmem_limit_bytes=VMEM_LIMIT)


def _ada_kernel(c_ref, w_ref, b_ref, o_ref):
    c = c_ref[...]
    a = c * jax.nn.sigmoid(c)
    o_ref[...] = jnp.dot(a, w_ref[...], preferred_element_type=F32) + b_ref[...]


def _ada(cc, w, b):
    m, k = cc.shape
    n = w.shape[1]
    tn = 1024
    return pl.pallas_call(
        _ada_kernel,
        grid=(n // tn,),
        in_specs=[pl.BlockSpec((m, k), lambda j: (0, 0)),
                  pl.BlockSpec((k, tn), lambda j: (0, j)),
                  pl.BlockSpec((1, tn), lambda j: (0, j))],
        out_specs=pl.BlockSpec((m, tn), lambda j: (0, j)),
        out_shape=jax.ShapeDtypeStruct((m, n), F32),
        compiler_params=_cparams(("parallel",)),
        name="ada",
    )(cc, w, b)


def _rms_rows(x, g):
    return x * lax.rsqrt(jnp.mean(x * x, axis=-1, keepdims=True) + EPS) * g


def _in_proj_kernel(x_ref, g_ref, sh_ref, sc_ref, w_ref, o_ref, h_ref, *, chunk):
    @pl.when(pl.program_id(1) == 0)
    def _():
        g = g_ref[...]
        sh = sh_ref[...]
        sc1 = 1.0 + sc_ref[...]

        def body(r, carry):
            rows = pl.ds(pl.multiple_of(r * chunk, chunk), chunk)
            h_ref[rows, :] = (_rms_rows(x_ref[rows, :], g) * sc1 + sh).astype(BF16)
            return carry

        lax.fori_loop(0, x_ref.shape[0] // chunk, body, 0)

    o_ref[...] = jnp.dot(h_ref[...], w_ref[...], preferred_element_type=F32).astype(BF16)


def _in_proj(x2d, g, shift, scale, w, tiles_per_mod, tm, tn):
    m, k = x2d.shape
    n = w.shape[1]
    return pl.pallas_call(
        functools.partial(_in_proj_kernel, chunk=128),
        grid=(m // tm, n // tn),
        in_specs=[pl.BlockSpec((tm, k), lambda i, j: (i, 0)),
                  pl.BlockSpec((1, k), lambda i, j: (0, 0)),
                  pl.BlockSpec((None, 1, k), lambda i, j: (i // tiles_per_mod, 0, 0)),
                  pl.BlockSpec((None, 1, k), lambda i, j: (i // tiles_per_mod, 0, 0)),
                  pl.BlockSpec((k, tn), lambda i, j: (0, j))],
        out_specs=pl.BlockSpec((tm, tn), lambda i, j: (i, j)),
        out_shape=jax.ShapeDtypeStruct((m, n), BF16),
        scratch_shapes=[pltpu.VMEM((tm, k), BF16)],
        compiler_params=_cparams(("parallel", "arbitrary")),
        name="in_proj",
    )(x2d, g, shift, scale, w)


def _mla_q_kernel(cq_ref, g_ref, w_ref, cos_ref, sin_ref, qn_ref, qp_ref):
    h = _rms_rows(cq_ref[...].astype(F32), g_ref[...]).astype(BF16)
    q = jnp.dot(h, w_ref[...], preferred_element_type=F32)
    nw = MLA_HEADS * MLA_NOPE
    rw = MLA_HEADS * MLA_ROPE
    qn_ref[...] = (q[:, :nw] * MLA_SCALE).astype(BF16)
    roped = q[:, nw:nw + rw] * cos_ref[...] + q[:, nw + rw:] * sin_ref[...]
    qp_ref[...] = (roped * MLA_SCALE).astype(BF16)


def _mla_q(p, col_block, g, w, cos_t, sin_t, seq, tm):
    t = p.shape[0]
    per = seq // tm
    nw = MLA_HEADS * MLA_NOPE
    rw = MLA_HEADS * MLA_ROPE
    return pl.pallas_call(
        _mla_q_kernel,
        grid=(t // tm,),
        in_specs=[pl.BlockSpec((tm, MLA_Q_RANK), lambda i: (i, col_block)),
                  pl.BlockSpec((1, MLA_Q_RANK), lambda i: (0, 0)),
                  pl.BlockSpec(w.shape, lambda i: (0, 0)),
                  pl.BlockSpec((tm, rw), lambda i: (i % per, 0)),
                  pl.BlockSpec((tm, rw), lambda i: (i % per, 0))],
        out_specs=[pl.BlockSpec((tm, nw), lambda i: (i, 0)),
                   pl.BlockSpec((tm, rw), lambda i: (i, 0))],
        out_shape=[jax.ShapeDtypeStruct((t, nw), BF16), jax.ShapeDtypeStruct((t, rw), BF16)],
        compiler_params=_cparams(("parallel",)),
        name="mla_q",
    )(p, g, w, cos_t, sin_t)


def _mla_kv_kernel(ckv_ref, kpe_ref, g_ref, w_ref, tab_ref, kn_ref, v_ref, kp_ref):
    h = _rms_rows(ckv_ref[...].astype(F32), g_ref[...]).astype(BF16)
    kv = jnp.dot(h, w_ref[...], preferred_element_type=F32)
    nw = MLA_HEADS * MLA_NOPE
    kn_ref[...] = kv[:, :nw].astype(BF16)
    v_ref[...] = kv[:, nw:].astype(BF16)
    t = kpe_ref[...].astype(F32) * tab_ref[...]
    kp_ref[...] = (t + pltpu.roll(t, MLA_ROPE, 1)).astype(BF16)


def _mla_kv(p, ckv_block, kpe_block, g, w, tab, seq, tm):
    t = p.shape[0]
    per = seq // tm
    nw = MLA_HEADS * MLA_NOPE
    vw = MLA_HEADS * MLA_V
    return pl.pallas_call(
        _mla_kv_kernel,
        grid=(t // tm,),
        in_specs=[pl.BlockSpec((tm, MLA_KV_RANK), lambda i: (i, ckv_block)),
                  pl.BlockSpec((tm, LANE), lambda i: (i, kpe_block)),
                  pl.BlockSpec((1, MLA_KV_RANK), lambda i: (0, 0)),
                  pl.BlockSpec(w.shape, lambda i: (0, 0)),
                  pl.BlockSpec((tm, LANE), lambda i: (i % per, 0))],
        out_specs=[pl.BlockSpec((tm, nw), lambda i: (i, 0)),
                   pl.BlockSpec((tm, vw), lambda i: (i, 0)),
                   pl.BlockSpec((tm, LANE), lambda i: (i, 0))],
        out_shape=[jax.ShapeDtypeStruct((t, nw), BF16), jax.ShapeDtypeStruct((t, vw), BF16),
                   jax.ShapeDtypeStruct((t, LANE), BF16)],
        compiler_params=_cparams(("parallel",)),
        name="mla_kv",
    )(p, p, g, w, tab)


def _na_group_plan(rows):
    kh = min(NA_WIN_H, rows)
    plan = []
    for g in range(rows // NA_GROUP_ROWS):
        start = int(np.clip(NA_GROUP_ROWS * g - NA_WIN_H // 2, 0, rows - NA_KEY_ROWS))
        per_row = []
        for j in range(NA_GROUP_ROWS):
            r = NA_GROUP_ROWS * g + j
            r0 = int(np.clip(r - NA_WIN_H // 2, 0, rows - kh))
            assert start <= r0 and r0 + kh <= start + NA_KEY_ROWS
            per_row.append((r, r0, r0 + kh - 1))
        plan.append((start, per_row))
    return plan


def _na_kernel(q_ref, k_ref, v_ref, kc_ref, vc_ref, bias_ref, o_ref, s_ref, sc_ref, p_ref, pc_ref, l_ref,
               *, rows):
    gq = NA_GROUP_ROWS * GRID_W
    gk = NA_KEY_ROWS * GRID_W
    lane_lo = lax.broadcasted_iota(jnp.int32, (1, LANE), 1) < NA_HEAD_DIM
    half_lo = lax.broadcasted_iota(jnp.int32, (GRID_W, LANE), 1) < GRID_W
    kc = kc_ref[...]
    vc = vc_ref[...]
    contract_last = (((1,), (1,)), ((), ()))
    for g, (start, per_row) in enumerate(_na_group_plan(rows)):
        q2 = q_ref[g * gq:(g + 1) * gq, :] * NA_SCALE
        kwin = k_ref[start * GRID_W:start * GRID_W + gk, :]
        vwin = v_ref[start * GRID_W:start * GRID_W + gk, :]
        outs = []
        for half in range(2):
            keep = lane_lo if half == 0 else jnp.logical_not(lane_lo)
            qh = jnp.where(keep, q2, jnp.zeros_like(q2))
            s_ref[...] = lax.dot_general(qh, kwin, contract_last, preferred_element_type=F32)
            sc_ref[...] = lax.dot_general(qh, kc, contract_last, preferred_element_type=F32)
            for j, (r, lo, hi) in enumerate(per_row):
                rs = slice(j * GRID_W, (j + 1) * GRID_W)
                blocks = []
                for m in range(NA_KEY_ROWS // 2):
                    ra, rb = start + 2 * m, start + 2 * m + 1
                    va, vb = lo <= ra <= hi, lo <= rb <= hi
                    if not (va or vb):
                        blocks.append(None)
                        continue
                    dy_a = ra - r + NA_WIN_H - 1
                    blk = s_ref[rs, m * LANE:(m + 1) * LANE] + bias_ref[half, dy_a + 1]
                    if not va:
                        blk = jnp.where(half_lo, NEG_INF, blk)
                    if not vb:
                        blk = jnp.where(half_lo, blk, NEG_INF)
                    blocks.append(blk)
                sctx = sc_ref[rs, :]
                mx = jnp.max(sctx, axis=-1, keepdims=True)
                for blk in blocks:
                    if blk is not None:
                        mx = jnp.maximum(mx, jnp.max(blk, axis=-1, keepdims=True))
                pctx = jnp.exp(sctx - mx)
                den = jnp.sum(pctx, axis=-1, keepdims=True)
                pc_ref[rs, :] = pctx.astype(BF16)
                for m, blk in enumerate(blocks):
                    cs = slice(m * LANE, (m + 1) * LANE)
                    if blk is None:
                        p_ref[rs, cs] = jnp.zeros((GRID_W, LANE), BF16)
                    else:
                        e = jnp.exp(blk - mx)
                        den = den + jnp.sum(e, axis=-1, keepdims=True)
                        p_ref[rs, cs] = e.astype(BF16)
                l_ref[rs, :] = jnp.broadcast_to(den, (GRID_W, LANE))
            o = (jnp.dot(p_ref[...], vwin, preferred_element_type=F32)
                 + jnp.dot(pc_ref[...], vc, preferred_element_type=F32))
            outs.append(o / l_ref[...])
        o_ref[g * gq:(g + 1) * gq, :] = jnp.where(lane_lo, outs[0], outs[1]).astype(BF16)


def _na_attn(p, pc, bias, batch, seq, q_blk, k_blk, v_blk, kc_blk, vc_blk):
    rows = seq // GRID_W
    ctx_len = pc.shape[0] // batch
    gq = NA_GROUP_ROWS * GRID_W
    gk = NA_KEY_ROWS * GRID_W
    return pl.pallas_call(
        functools.partial(_na_kernel, rows=rows),
        grid=(batch, NA_PAIRS),
        in_specs=[pl.BlockSpec((seq, LANE), lambda b, h: (b, q_blk + h)),
                  pl.BlockSpec((seq, LANE), lambda b, h: (b, k_blk + h)),
                  pl.BlockSpec((seq, LANE), lambda b, h: (b, v_blk + h)),
                  pl.BlockSpec((ctx_len, LANE), lambda b, h: (b, kc_blk + h)),
                  pl.BlockSpec((ctx_len, LANE), lambda b, h: (b, vc_blk + h)),
                  pl.BlockSpec((2,) + bias.shape[1:], lambda b, h: (h, 0, 0, 0))],
        out_specs=pl.BlockSpec((seq, LANE), lambda b, h: (b, h)),
        out_shape=jax.ShapeDtypeStruct((batch * seq, NA_WIDTH), BF16),
        scratch_shapes=[pltpu.VMEM((gq, gk), F32), pltpu.VMEM((gq, ctx_len), F32),
                        pltpu.VMEM((gq, gk), BF16), pltpu.VMEM((gq, ctx_len), BF16),
                        pltpu.VMEM((gq, LANE), F32)],
        compiler_params=_cparams(("parallel", "parallel")),
        name="na_attn",
    )(p, p, p, pc, pc, bias)


def _mla_attn_kernel(qn_ref, qp_ref, kn_ref, kp_ref, v_ref, knc_ref, kpc_ref, vc_ref, o_ref):
    lane_lo = lax.broadcasted_iota(jnp.int32, (1, LANE), 1) < MLA_ROPE
    contract_last = (((1,), (1,)), ((), ()))
    kp = kp_ref[...]
    kpc = kpc_ref[...]
    for h in range(MLA_HEADS):
        cs = slice(h * LANE, (h + 1) * LANE)
        qn = qn_ref[:, cs]
        qp2 = qp_ref[:, (h // 2) * LANE:(h // 2 + 1) * LANE]
        keep = lane_lo if h % 2 == 0 else jnp.logical_not(lane_lo)
        qp = jnp.where(keep, qp2, jnp.zeros_like(qp2))
        s = (lax.dot_general(qn, kn_ref[:, cs], contract_last, preferred_element_type=F32)
             + lax.dot_general(qp, kp, contract_last, preferred_element_type=F32))
        sc = (lax.dot_general(qn, knc_ref[:, cs], contract_last, preferred_element_type=F32)
              + lax.dot_general(qp, kpc, contract_last, preferred_element_type=F32))
        mx = jnp.maximum(jnp.max(s, axis=-1, keepdims=True), jnp.max(sc, axis=-1, keepdims=True))
        e = jnp.exp(s - mx)
        ec = jnp.exp(sc - mx)
        den = jnp.sum(e, axis=-1, keepdims=True) + jnp.sum(ec, axis=-1, keepdims=True)
        o = (jnp.dot(e.astype(BF16), v_ref[:, cs], preferred_element_type=F32)
             + jnp.dot(ec.astype(BF16), vc_ref[:, cs], preferred_element_type=F32))
        o_ref[:, cs] = (o / den).astype(BF16)


def _mla_attn(qn, qp, kn, kp, v, knc, kpc, vc, batch, seq, tq):
    ctx_len = knc.shape[0] // batch
    per = seq // tq
    w = MLA_WIDTH
    return pl.pallas_call(
        _mla_attn_kernel,
        grid=(batch, per),
        in_specs=[pl.BlockSpec((tq, w), lambda b, i: (b * per + i, 0)),
                  pl.BlockSpec((tq, qp.shape[1]), lambda b, i: (b * per + i, 0)),
                  pl.BlockSpec((seq, w), lambda b, i: (b, 0)),
                  pl.BlockSpec((seq, LANE), lambda b, i: (b, 0)),
                  pl.BlockSpec((seq, w), lambda b, i: (b, 0)),
                  pl.BlockSpec((ctx_len, w), lambda b, i: (b, 0)),
                  pl.BlockSpec((ctx_len, LANE), lambda b, i: (b, 0)),
                  pl.BlockSpec((ctx_len, w), lambda b, i: (b, 0))],
        out_specs=pl.BlockSpec((tq, w), lambda b, i: (b * per + i, 0)),
        out_shape=jax.ShapeDtypeStruct((batch * seq, w), BF16),
        compiler_params=_cparams(("parallel", "parallel")),
        name="mla_attn",
    )(qn, qp, kn, kp, v, knc, kpc, vc)


def _merge_kernel(oa_ref, ob_ref, ga_ref, gb_ref, x_ref, wa_ref, wb_ref, wo_ref,
                  gpost_ref, gt1_ref, gpre_ref, sh2_ref, sc2_ref, x1_ref, h2t_ref):
    ya = jnp.dot(oa_ref[...], wa_ref[...], preferred_element_type=F32)
    yb = jnp.dot(ob_ref[...], wb_ref[...], preferred_element_type=F32)
    y = (jax.nn.sigmoid(ga_ref[...].astype(F32)) * ya + jax.nn.sigmoid(gb_ref[...].astype(F32)) * yb)
    z = jnp.dot(y.astype(BF16), wo_ref[...], preferred_element_type=F32)
    x1 = x_ref[...] + gt1_ref[...] * _rms_rows(z, gpost_ref[...])
    x1_ref[...] = x1
    h2 = _rms_rows(x1, gpre_ref[...]) * (1.0 + sc2_ref[...]) + sh2_ref[...]
    h2t_ref[...] = h2.T.astype(BF16)


def _merge(oa, ob, p, ga_blk, gb_blk, x2d, wa, wb, wo, gpost, gt1, gpre, sh2, sc2, seq, tm):
    t, d = x2d.shape
    per = seq // tm
    const = lambda i: (0, 0)
    mod = lambda i: (i // per, 0, 0)
    return pl.pallas_call(
        _merge_kernel,
        grid=(t // tm,),
        in_specs=[pl.BlockSpec((tm, oa.shape[1]), lambda i: (i, 0)),
                  pl.BlockSpec((tm, ob.shape[1]), lambda i: (i, 0)),
                  pl.BlockSpec((tm, d), lambda i: (i, ga_blk)),
                  pl.BlockSpec((tm, d), lambda i: (i, gb_blk)),
                  pl.BlockSpec((tm, d), lambda i: (i, 0)),
                  pl.BlockSpec(wa.shape, const, pipeline_mode=pl.Buffered(1)),
                  pl.BlockSpec(wb.shape, const, pipeline_mode=pl.Buffered(1)),
                  pl.BlockSpec(wo.shape, const, pipeline_mode=pl.Buffered(1)),
                  pl.BlockSpec((1, d), const),
                  pl.BlockSpec((None, 1, d), mod),
                  pl.BlockSpec((1, d), const),
                  pl.BlockSpec((None, 1, d), mod),
                  pl.BlockSpec((None, 1, d), mod)],
        out_specs=[pl.BlockSpec((tm, d), lambda i: (i, 0)),
                   pl.BlockSpec((d, tm), lambda i: (0, i))],
        out_shape=[jax.ShapeDtypeStruct((t, d), F32), jax.ShapeDtypeStruct((d, t), BF16)],
        compiler_params=_cparams(("parallel",)),
        name="merge",
    )(oa, ob, p, p, x2d, wa, wb, wo, gpost, gt1, gpre, sh2, sc2)


def _peer_scores_kernel(ht_ref, wq_ref, sub_ref, o_ref):
    qt = jnp.dot(wq_ref[...], ht_ref[...], preferred_element_type=F32).astype(BF16)
    half = PEER_KEY_DIM // 2
    for hp in range(2 * PEER_HEADS):
        o_ref[hp] = jnp.dot(sub_ref[hp], qt[hp * half:(hp + 1) * half, :], preferred_element_type=F32)


def _peer_scores(h2t, wq_t, sub, tm):
    d, t = h2t.shape
    n_hp = sub.shape[0]
    return pl.pallas_call(
        _peer_scores_kernel,
        grid=(t // tm,),
        in_specs=[pl.BlockSpec((d, tm), lambda i: (0, i)),
                  pl.BlockSpec(wq_t.shape, lambda i: (0, 0)),
                  pl.BlockSpec(sub.shape, lambda i: (0, 0, 0))],
        out_specs=pl.BlockSpec((n_hp, PEER_KEYS, tm), lambda i: (0, 0, i)),
        out_shape=jax.ShapeDtypeStruct((n_hp, PEER_KEYS, t), F32),
        compiler_params=_cparams(("parallel",)),
        name="peer_scores",
    )(h2t, wq_t, sub)


_TOPN = PEER_TOPK + 1
_VROWS = 24


def _sorted_top(s, row_id):
    out = jnp.full((_VROWS, s.shape[1]), -jnp.inf, F32)
    for it in range(_TOPN):
        m = jnp.max(s, axis=0, keepdims=True)
        out = jnp.where(row_id == it, m, out)
        if it + 1 < _TOPN:
            s = jnp.where(s >= m, -jnp.inf, s)
    return out


def _peer_topk_kernel(sc_ref, ad_ref, b_ref, s2_ref):
    lanes = sc_ref.shape[-1]
    row24 = lax.broadcasted_iota(jnp.int32, (_VROWS, lanes), 0)
    row8 = lax.broadcasted_iota(jnp.int32, (8, lanes), 0)
    for h in range(PEER_HEADS):
        s1 = sc_ref[2 * h]
        s2 = sc_ref[2 * h + 1]
        v1 = _sorted_top(s1, row24)
        v2 = _sorted_top(s2, row24)
        cands = [v1[0:1] + v2]
        for j in range(1, 8):
            cands.append(jnp.where(row8 < _TOPN // (j + 1), v1[j:j + 1] + v2[0:8], -jnp.inf))
        cands.append(jnp.where(row24 >= 8, v1 + v2[0:1], -jnp.inf))
        cand = jnp.concatenate(cands, axis=0)
        work = cand
        for it in range(_TOPN):
            m = jnp.max(work, axis=0, keepdims=True)
            if it == PEER_TOPK - 1:
                p16 = m
            if it + 1 < _TOPN:
                work = jnp.where(work >= m, -jnp.inf, work)
        tau = 0.5 * (p16 + m)
        m1 = v1[0:1]
        m2 = v2[0:1]
        z = jnp.sum(jnp.where(cand > tau, jnp.exp(cand - (m1 + m2)), 0.0), axis=0, keepdims=True)
        ad_ref[h] = jnp.exp(s1 - m1)
        ad_ref[PEER_HEADS + h] = tau - s1
        bb = jnp.exp(s2 - m2) / z
        for lg in range(lanes // LANE):
            b_ref[h, lg] = bb[:, lg * LANE:(lg + 1) * LANE]
            s2_ref[h, lg] = s2[:, lg * LANE:(lg + 1) * LANE]


def _peer_topk(sc, tl):
    n_hp, nk, t = sc.shape
    out = jax.ShapeDtypeStruct((PEER_HEADS, t // LANE, nk, LANE), F32)
    spec = pl.BlockSpec((PEER_HEADS, tl // LANE, nk, LANE), lambda i: (0, i, 0, 0))
    out2 = jax.ShapeDtypeStruct((2 * PEER_HEADS, nk, t), F32)
    spec2 = pl.BlockSpec((2 * PEER_HEADS, nk, tl), lambda i: (0, 0, i))
    return pl.pallas_call(
        _peer_topk_kernel,
        grid=(t // tl,),
        in_specs=[pl.BlockSpec((n_hp, nk, tl), lambda i: (0, 0, i))],
        out_specs=[spec2, spec, spec],
        out_shape=[out2, out, out],
        compiler_params=_cparams(("parallel",)),
        name="peer_topk",
    )(sc)


def _gelu_exact(x):
    return 0.5 * x * (1.0 + lax.erf(x * math.sqrt(0.5)))


def _peer_dense_kernel(ht_ref, u_ref, vt_ref, ad_ref, b_ref, s2_ref, o_ref, st_ref, at_ref):
    te = u_ref.shape[0]
    tm = ht_ref.shape[1]

    @pl.when(pl.program_id(1) == 0)
    def _():
        o_ref[...] = jnp.zeros_like(o_ref)

    st_ref[...] = jnp.dot(u_ref[...], ht_ref[...], preferred_element_type=F32)
    for k in range(te // PEER_KEYS):
        rows = slice(k * PEER_KEYS, (k + 1) * PEER_KEYS)
        ad = ad_ref[k]
        for lg in range(tm // LANE):
            lanes = slice(lg * LANE, (lg + 1) * LANE)
            gate = jnp.zeros((PEER_KEYS, LANE), F32)
            for h in range(PEER_HEADS):
                a_row = ad[h:h + 1, lanes]
                d_row = ad[PEER_HEADS + h:PEER_HEADS + h + 1, lanes]
                gate = gate + jnp.where(s2_ref[h, lg] > d_row, b_ref[h, lg], 0.0) * a_row
            at_ref[rows, lanes] = (gate * _gelu_exact(st_ref[rows, lanes])).astype(BF16)
    o_ref[...] += jnp.dot(vt_ref[...], at_ref[...], preferred_element_type=F32)


def _peer_dense(h2t, u, vt, ad, b, s2, tm, te):
    d, t = h2t.shape
    stat = pl.BlockSpec((PEER_HEADS, tm // LANE, PEER_KEYS, LANE), lambda i, e: (0, i, 0, 0))
    return pl.pallas_call(
        _peer_dense_kernel,
        grid=(t // tm, u.shape[0] // te),
        in_specs=[pl.BlockSpec((d, tm), lambda i, e: (0, i)),
                  pl.BlockSpec((te, d), lambda i, e: (e, 0)),
                  pl.BlockSpec((d, te), lambda i, e: (0, e)),
                  pl.BlockSpec((te // PEER_KEYS, 2 * PEER_HEADS, tm), lambda i, e: (e, 0, i)),
                  stat, stat],
        out_specs=pl.BlockSpec((d, tm), lambda i, e: (0, i)),
        out_shape=jax.ShapeDtypeStruct((d, t), F32),
        scratch_shapes=[pltpu.VMEM((te, tm), F32), pltpu.VMEM((te, tm), BF16)],
        compiler_params=_cparams(("parallel", "arbitrary")),
        name="peer_dense",
    )(h2t, u, vt, ad, b, s2)


def _peer_out_kernel(pt_ref, x1_ref, g_ref, gt2_ref, o_ref):
    y = pt_ref[...].T
    o_ref[...] = x1_ref[...] + gt2_ref[...] * _rms_rows(y, g_ref[...])


def _peer_out(peer_t, x1, g, gt2, seq, tm):
    t, d = x1.shape
    per = seq // tm
    return pl.pallas_call(
        _peer_out_kernel,
        grid=(t // tm,),
        in_specs=[pl.BlockSpec((d, tm), lambda i: (0, i)),
                  pl.BlockSpec((tm, d), lambda i: (i, 0)),
                  pl.BlockSpec((1, d), lambda i: (0, 0)),
                  pl.BlockSpec((None, 1, d), lambda i: (i // per, 0, 0))],
        out_specs=pl.BlockSpec((tm, d), lambda i: (i, 0)),
        out_shape=jax.ShapeDtypeStruct((t, d), F32),
        compiler_params=_cparams(("parallel",)),
        name="peer_out",
    )(peer_t, x1, g, gt2)


def _rope_tables(seq):
    t = jnp.arange(seq)
    row = (t // GRID_W).astype(F32)
    col = (t % GRID_W).astype(F32)
    quarter = MLA_ROPE // 4
    inv_freq = ROPE_THETA ** (-jnp.arange(quarter, dtype=F32) / quarter)
    ang = jnp.concatenate([row[:, None] * inv_freq, col[:, None] * inv_freq], axis=-1)
    cos, sin = jnp.cos(ang), jnp.sin(ang)
    return jnp.concatenate([cos, cos], axis=-1), jnp.concatenate([-sin, sin], axis=-1)


def _swap_halves(w, width):
    lead = w.shape[0]
    g = w.reshape(lead, -1, 2, width // 2)
    return g[:, :, ::-1, :].reshape(lead, -1)


def _na_bias_table(rpb):
    cols = np.arange(GRID_W)
    win_start = np.clip(cols - NA_WIN_W // 2, 0, GRID_W - NA_WIN_W)
    off = cols[None, :] - win_start[:, None]
    col_ok = (off >= 0) & (off < NA_WIN_W)
    dx = np.clip(cols[None, :] - cols[:, None] + NA_WIN_W - 1, 0, 2 * NA_WIN_W - 2)
    t = jnp.where(col_ok[None, None], rpb[:, :, dx], NEG_INF)
    neg = jnp.full((rpb.shape[0], 1, GRID_W, GRID_W), NEG_INF, F32)
    t = jnp.concatenate([neg, t, neg], axis=1)
    return jnp.concatenate([t[:, :-1], t[:, 1:]], axis=-1)


def kernel(x, c, ctx, c_ctx, w_ada, b_ada, g_pre_mix, g_post_mix, g_pre_ffn, g_post_ffn, w_in, na_rpb,
           mla_g_q, mla_w_q_b, mla_g_kv, mla_w_kv_b, w_branch_a, w_branch_b, w_out,
           peer_w_query, peer_subkeys, peer_u, peer_v):
    batch, seq, d = x.shape
    ctx_len = ctx.shape[1]
    assert w_ada.shape[0] == 1 and seq % (GRID_W * NA_GROUP_ROWS) == 0 and seq // GRID_W >= NA_KEY_ROWS
    t = batch * seq

    n_mod = batch + 1
    pad = (-n_mod) % 8
    cc = jnp.concatenate([c, c_ctx[None], jnp.zeros((pad, d), F32)], axis=0)
    mod = _ada(cc, w_ada[0], b_ada[0][None])
    sh1, sc1, gt1, sh2, sc2, gt2 = [m[:, None, :] for m in jnp.split(mod[:n_mod], 6, axis=-1)]

    w = w_in[0]
    o = np.cumsum((0, NA_WIDTH, NA_WIDTH, NA_WIDTH, MLA_Q_RANK, MLA_KV_RANK, MLA_ROPE, d, d))
    qa, ka, va, cq, ckv, kpe, ga, gb = [w[:, o[i]:o[i + 1]] for i in range(8)]
    kpe2 = jnp.concatenate([kpe, _swap_halves(kpe, MLA_ROPE)], axis=1)
    zpad = lambda n: jnp.zeros((d, n), F32)
    w_lat = jnp.concatenate([ga, gb, qa, ka, va, ckv, cq, kpe2, zpad(LANE)], axis=1).astype(BF16)
    w_ctx = jnp.concatenate([ka, va, ckv, kpe2, zpad(3 * LANE)], axis=1).astype(BF16)
    lat = dict(ga=0, gb=1, q=(2 * d) // LANE, k=(2 * d + NA_WIDTH) // LANE, v=(2 * d + 2 * NA_WIDTH) // LANE,
               ckv=(2 * d + 3 * NA_WIDTH) // MLA_KV_RANK,
               cq=(2 * d + 3 * NA_WIDTH + MLA_KV_RANK) // MLA_Q_RANK,
               kpe=(2 * d + 3 * NA_WIDTH + MLA_KV_RANK + MLA_Q_RANK) // LANE)
    assert (2 * d + 3 * NA_WIDTH + MLA_KV_RANK) % MLA_Q_RANK == 0
    cx = dict(k=0, v=NA_WIDTH // LANE, ckv=(2 * NA_WIDTH) // MLA_KV_RANK, kpe=(2 * NA_WIDTH + MLA_KV_RANK) // LANE)

    g_pre = g_pre_mix[0][None]
    p_lat = _in_proj(x.reshape(t, d), g_pre, sh1[:batch], sc1[:batch], w_lat, seq // 1024, 1024, 512)
    p_ctx = _in_proj(ctx.reshape(batch * ctx_len, d), g_pre, sh1[batch:], sc1[batch:], w_ctx,
                     batch * ctx_len // 1024 if batch * ctx_len >= 1024 else 1,
                     min(1024, batch * ctx_len), 512)

    c64, s64 = _rope_tables(seq)
    wq = mla_w_q_b[0].reshape(MLA_Q_RANK, MLA_HEADS, MLA_NOPE + MLA_ROPE)
    wq_pe = wq[:, :, MLA_NOPE:].reshape(MLA_Q_RANK, -1)
    wq2 = jnp.concatenate([wq[:, :, :MLA_NOPE].reshape(MLA_Q_RANK, -1), wq_pe,
                           _swap_halves(wq_pe, MLA_ROPE)], axis=1).astype(BF16)
    wkv = mla_w_kv_b[0].reshape(MLA_KV_RANK, MLA_HEADS, MLA_NOPE + MLA_V)
    wkv2 = jnp.concatenate([wkv[:, :, :MLA_NOPE].reshape(MLA_KV_RANK, -1),
                            wkv[:, :, MLA_NOPE:].reshape(MLA_KV_RANK, -1)], axis=1).astype(BF16)
    qn, qp = _mla_q(p_lat, lat["cq"], mla_g_q, wq2, jnp.tile(c64, (1, MLA_HEADS)),
                    jnp.tile(s64, (1, MLA_HEADS)), seq, 512)
    kn, vv, kp = _mla_kv(p_lat, lat["ckv"], lat["kpe"], mla_g_kv, wkv2,
                         jnp.concatenate([c64, s64], axis=1), seq, 512)
    tab_ctx = jnp.concatenate([jnp.ones((ctx_len, MLA_ROPE), F32), jnp.zeros((ctx_len, MLA_ROPE), F32)], axis=1)
    knc, vvc, kpc = _mla_kv(p_ctx, cx["ckv"], cx["kpe"], mla_g_kv, wkv2, tab_ctx, ctx_len, ctx_len)

    oa = _na_attn(p_lat, p_ctx, _na_bias_table(na_rpb[0]), batch, seq,
                  lat["q"], lat["k"], lat["v"], cx["k"], cx["v"])
    ob = _mla_attn(qn, qp, kn, kp, vv, knc, kpc, vvc, batch, seq, 256)
    x1, h2t = _merge(oa, ob, p_lat, lat["ga"], lat["gb"], x.reshape(t, d),
                    w_branch_a[0].astype(BF16), w_branch_b[0].astype(BF16), w_out[0].astype(BF16),
                    g_post_mix[0][None], gt1[:batch], g_pre_ffn[0][None], sh2[:batch], sc2[:batch], seq, 256)

    sub = peer_subkeys[0].reshape(2 * PEER_HEADS, PEER_KEYS, PEER_KEY_DIM // 2).astype(BF16)
    sc = _peer_scores(h2t, peer_w_query[0].T.astype(BF16), sub, 512)
    ad, b, s2 = _peer_topk(sc, 256)
    ad = ad.transpose(1, 0, 2)
    peer_t = _peer_dense(h2t, peer_u[0].astype(BF16), peer_v[0].T.astype(BF16), ad, b, s2, 512, 512)
    out = _peer_out(peer_t, x1, g_post_ffn[0][None], gt2[:batch], seq, 256)
    return out.reshape(batch, seq, d)
```
